```python
import jax, jax.numpy as jnp
from jax import lax
import numpy as np

D_MODEL = 1024
BATCH = 8
SEQ = 2048
DEPTH = 1
DEC_BATCH = 128
DEC_SEQ = 4
PAST_LEN = 16384
PAGE_SIZE = 128

CHUNK = 128
D_A = D_MODEL
GROUPS_A = 8
GW_A = D_A // GROUPS_A
HEAD_B = 64
D_B = D_MODEL
H_B = D_B // HEAD_B
DECAY_LORA = 64
ICLR_LORA = 64
GATE_LORA = 128
H_C = 4
D_C = D_MODEL
DH_C = D_C // H_C
N_MEM = 256
D_FF = 4 * D_MODEL
N_BRANCH = 3
P_SHIFT = 3 * D_B + DECAY_LORA + ICLR_LORA + GATE_LORA
P_IN = 2 * D_A + P_SHIFT + D_C + N_BRANCH * D_MODEL
RMS_EPS = 1e-6
LN_EPS = 1e-5
GN_EPS = 64e-5

kernel_name = "gmlp_rwkv7_memxattn_hybrid_step"


def _split(z, widths):
    outs, o = [], 0
    for w in widths:
        outs.append(z[..., o:o + w])
        o += w
    return outs


def rmsnorm(x, g):
    xf = x.astype(jnp.float32)
    y = xf * lax.rsqrt(jnp.mean(xf * xf, axis=-1, keepdims=True) + RMS_EPS)
    return (y * g.astype(jnp.float32)).astype(x.dtype)


def layernorm(x, g, b):
    xf = x.astype(jnp.float32)
    mu = jnp.mean(xf, axis=-1, keepdims=True)
    var = jnp.mean(jnp.square(xf - mu), axis=-1, keepdims=True)
    y = (xf - mu) * lax.rsqrt(var + LN_EPS)
    return (y * g.astype(jnp.float32) + b.astype(jnp.float32)).astype(x.dtype)


def spatial_gate(vn, sg_w, sg_b):
    B, T, _ = vn.shape
    nc = -(-T // CHUNK)
    vp = jnp.pad(vn, ((0, 0), (0, nc * CHUNK - T), (0, 0))).reshape(B, nc, CHUNK, GROUPS_A, GW_A)
    mask = jnp.tril(jnp.ones((CHUNK, CHUNK), dtype=bool))
    w = jnp.where(mask[None], sg_w, jnp.zeros((), sg_w.dtype))
    s = jnp.einsum('gts,bcsgh->bctgh', w, vp) + sg_b.T[None, None, :, :, None]
    return s.reshape(B, nc * CHUNK, D_A)[:, :T]


def wkv7_scan(r, decay, k, v, a_vec, b_vec, s0):
    def step(S, inp):
        r_t, w_t, k_t, v_t, a_t, b_t = inp
        sa = jnp.einsum('bhij,bhj->bhi', S, a_t)
        S = S * w_t[:, :, None, :] + sa[..., None] * b_t[:, :, None, :] + v_t[..., None] * k_t[:, :, None, :]
        return S, jnp.einsum('bhij,bhj->bhi', S, r_t)
    xs = tuple(jnp.moveaxis(t, 1, 0) for t in (r, decay, k, v, a_vec, b_vec))
    S, ys = lax.scan(step, s0, xs)
    return jnp.moveaxis(ys, 0, 1), S


def rwkv7_branch(rz, kz, vz, wl, al, gl, wkv0, p):
    B, T, _ = rz.shape
    f32 = jnp.float32
    w_log = -jax.nn.softplus(-(p['w0'] + jnp.tanh(wl) @ p['w_w2']).astype(f32)) - 0.5
    decay = jnp.exp(-jnp.exp(w_log))
    a = jax.nn.sigmoid((p['a0'] + al @ p['w_a2']).astype(f32))
    g = jax.nn.sigmoid(gl) @ p['w_g2']

    def heads(t):
        return t.astype(f32).reshape(B, T, H_B, HEAD_B)

    kk = heads(kz * p['k_k'])
    kk = kk * lax.rsqrt(jnp.sum(kk * kk, axis=-1, keepdims=True) + 1e-12)
    k = kz.astype(f32) * (1.0 + (a - 1.0) * p['k_a'].astype(f32))
    r_h, k_h, v_h, a_h = heads(rz), heads(k), heads(vz), heads(a)
    y, S = wkv7_scan(r_h, heads(decay), k_h, v_h, -kk, kk * a_h, wkv0.astype(f32))
    mu = jnp.mean(y, axis=-1, keepdims=True)
    var = jnp.mean(jnp.square(y - mu), axis=-1, keepdims=True)
    y = (y - mu) * lax.rsqrt(var + GN_EPS)
    y = y * p['lnx_g'].astype(f32).reshape(H_B, HEAD_B) + p['lnx_b'].astype(f32).reshape(H_B, HEAD_B)
    y = y + jnp.sum(r_h * k_h * p['r_k'].astype(f32), axis=-1, keepdims=True) * v_h
    y = y.reshape(B, T, D_B).astype(rz.dtype) * g
    return y, S


def memory_kv(mem, g, wk, wv):
    B, M, _ = mem.shape
    hm = rmsnorm(mem, g)
    return (hm @ wk).reshape(B, M, H_C, DH_C), (hm @ wv).reshape(B, M, H_C, DH_C)


def memory_cross_attn(qz, mk, mv):
    B, T, _ = qz.shape
    q = qz.reshape(B, T, H_C, DH_C)
    s = jnp.einsum('bthd,bmhd->bhtm', q, mk).astype(jnp.float32) * (DH_C ** -0.5)
    pr = jax.nn.softmax(s, axis=-1).astype(mv.dtype)
    return jnp.einsum('bhtm,bmhd->bthd', pr, mv).reshape(B, T, D_C)


def hybrid_layer(x, mk, mv, shift_prev, wkv_prev, p):
    B, T, _ = x.shape
    h = rmsnorm(x, p['norm1_g'])
    z = h @ p['w_in']
    uz, vz, sz, qz, gz = _split(z, (D_A, D_A, P_SHIFT, D_C, N_BRANCH * D_MODEL))
    prev = jnp.concatenate([shift_prev[:, None].astype(sz.dtype), sz[:, :-1]], axis=1)
    ss = sz + (prev - sz) * p['shift_mu']
    rz, kz, vbz, wl, al, gl = _split(ss, (D_B, D_B, D_B, DECAY_LORA, ICLR_LORA, GATE_LORA))
    u = jax.nn.gelu(uz)
    vn = layernorm(jax.nn.gelu(vz), p['ln_v_g'], p['ln_v_b'])
    y_a = u * spatial_gate(vn, p['sg_w'], p['sg_b'])
    y_b, wkv_new = rwkv7_branch(rz, kz, vbz, wl, al, gl, wkv_prev, p)
    y_c = memory_cross_attn(qz, mk, mv)
    branches = jnp.stack([y_a, y_b, y_c], axis=0)
    proj = jnp.einsum('nbtc,ncd->nbtd', branches, p['w_branch'])
    gates = jax.nn.sigmoid(gz.reshape(B, T, N_BRANCH, D_MODEL))
    merged = jnp.einsum('nbtd,btnd->btd', proj, gates)
    x = x + merged @ p['w_out']
    h2 = rmsnorm(x, p['norm2_g'])
    x = x + jnp.square(jax.nn.relu(h2 @ p['w_up'])) @ p['w_down']
    return x, sz[:, -1], wkv_new.astype(x.dtype), vn


def setup_inputs(seed: int = 0) -> dict:
    key = jax.random.key(seed)
    ks = iter(list(jax.random.split(key, 40)))
    f32 = jnp.float32
    L = DEPTH

    def nrm(shape, scale):
        return jax.random.normal(next(ks), shape, f32) * scale

    def gain(shape):
        return 1.0 + nrm(shape, 0.02)

    def unif(shape, lo, hi):
        return jax.random.uniform(next(ks), shape, f32, lo, hi)

    return {
        "x_prompt": nrm((BATCH, SEQ, D_MODEL), 1.0),
        "x_sample": nrm((DEC_BATCH, DEC_SEQ, D_MODEL), 1.0),
        "mem_prompt": nrm((BATCH, N_MEM, D_MODEL), 1.0),
        "cache_mem_k": nrm((L, DEC_BATCH, N_MEM, H_C, DH_C), 1.0),
        "cache_mem_v": nrm((L, DEC_BATCH, N_MEM, H_C, DH_C), 1.0),
        "state_shift": nrm((L, DEC_BATCH, P_SHIFT), 1.0),
        "state_wkv": nrm((L, DEC_BATCH, H_B, HEAD_B, HEAD_B), 0.1),
        "norm1_g": gain((L, D_MODEL)),
        "w_in": nrm((L, D_MODEL, P_IN), D_MODEL ** -0.5),
        "shift_mu": unif((L, P_SHIFT), 0.0, 1.0),
        "ln_v_g": gain((L, D_A)),
        "ln_v_b": nrm((L, D_A), 0.02),
        "sg_w": nrm((L, GROUPS_A, CHUNK, CHUNK), CHUNK ** -0.5),
        "sg_b": gain((L, GROUPS_A, CHUNK)),
        "w0": unif((L, D_B), -6.0, -1.0),
        "w_w2": nrm((L, DECAY_LORA, D_B), 0.5 * DECAY_LORA ** -0.5),
        "a0": nrm((L, D_B), 0.1),
        "w_a2": nrm((L, ICLR_LORA, D_B), ICLR_LORA ** -0.5),
        "w_g2": nrm((L, GATE_LORA, D_B), GATE_LORA ** -0.5),
        "k_k": 0.85 + nrm((L, D_B), 0.02),
        "k_a": gain((L, D_B)),
        "r_k": nrm((L, H_B, HEAD_B), 0.1),
        "lnx_g": gain((L, D_B)),
        "lnx_b": nrm((L, D_B), 0.02),
        "mem_norm_g": gain((L, D_MODEL)),
        "w_mem_k": nrm((L, D_MODEL, D_C), D_MODEL ** -0.5),
        "w_mem_v": nrm((L, D_MODEL, D_C), D_MODEL ** -0.5),
        "w_branch": nrm((L, N_BRANCH, D_MODEL, D_MODEL), D_MODEL ** -0.5),
        "w_out": nrm((L, D_MODEL, D_MODEL), D_MODEL ** -0.5),
        "norm2_g": gain((L, D_MODEL)),
        "w_up": nrm((L, D_MODEL, D_FF), D_MODEL ** -0.5),
        "w_down": nrm((L, D_FF, D_MODEL), D_FF ** -0.5),
        "norm_f_g": gain((D_MODEL,)),
    }


def reference(x_prompt, x_sample, mem_prompt, cache_mem_k, cache_mem_v, state_shift, state_wkv,
              norm1_g, w_in, shift_mu, ln_v_g, ln_v_b, sg_w, sg_b, w0, w_w2, a0, w_a2, w_g2,
              k_k, k_a, r_k, lnx_g, lnx_b, mem_norm_g, w_mem_k, w_mem_v, w_branch, w_out,
              norm2_g, w_up, w_down, norm_f_g):
    xp, xs = x_prompt, x_sample
    Bp = x_prompt.shape[0]
    p_shift, p_wkv, p_mk, p_mv = [], [], [], []
    s_shift, s_wkv, s_gv = [], [], []
    for l in range(DEPTH):
        p = dict(norm1_g=norm1_g[l], w_in=w_in[l], shift_mu=shift_mu[l], ln_v_g=ln_v_g[l],
                 ln_v_b=ln_v_b[l], sg_w=sg_w[l], sg_b=sg_b[l], w0=w0[l], w_w2=w_w2[l], a0=a0[l],
                 w_a2=w_a2[l], w_g2=w_g2[l], k_k=k_k[l], k_a=k_a[l], r_k=r_k[l], lnx_g=lnx_g[l],
                 lnx_b=lnx_b[l], w_branch=w_branch[l], w_out=w_out[l], norm2_g=norm2_g[l],
                 w_up=w_up[l], w_down=w_down[l])
        mk_p, mv_p = memory_kv(mem_prompt, mem_norm_g[l], w_mem_k[l], w_mem_v[l])
        shift0 = jnp.zeros((Bp, P_SHIFT), x_prompt.dtype)
        wkv0 = jnp.zeros((Bp, H_B, HEAD_B, HEAD_B), jnp.float32)
        xp, sh_p, wkv_p, _ = hybrid_layer(xp, mk_p, mv_p, shift0, wkv0, p)
        xs, sh_s, wkv_s, vn_s = hybrid_layer(xs, cache_mem_k[l], cache_mem_v[l],
                                             state_shift[l], state_wkv[l], p)
        p_shift.append(sh_p); p_wkv.append(wkv_p); p_mk.append(mk_p); p_mv.append(mv_p)
        s_shift.append(sh_s); s_wkv.append(wkv_s); s_gv.append(vn_s)
    y_prompt = rmsnorm(xp, norm_f_g)
    y_sample = rmsnorm(xs, norm_f_g)
    return (y_prompt, y_sample, jnp.stack(p_shift), jnp.stack(p_wkv), jnp.stack(p_mk),
            jnp.stack(p_mv), jnp.stack(s_shift), jnp.stack(s_wkv), jnp.stack(s_gv))
```

```python
import functools

import jax
import jax.numpy as jnp
from jax import lax
from jax.experimental import pallas as pl
from jax.experimental.pallas import tpu as pltpu

F32 = jnp.float32
BF16 = jnp.bfloat16

D_MODEL = 1024
GROUPS_A = 8
CHUNK_A = 128
HEAD_B = 64
H_B = D_MODEL // HEAD_B
LORA_WA = 128
GATE_LORA = 128
P_SHIFT = 3 * D_MODEL + LORA_WA + GATE_LORA
H_C = 4
DH_C = D_MODEL // H_C
D_FF = 4 * D_MODEL
N_BRANCH = 3
P_IN = 2 * D_MODEL + P_SHIFT + D_MODEL + N_BRANCH * D_MODEL
RMS_EPS = 1e-6
LN_EPS = 1e-5
GN_EPS = 64e-5

LANES = 128
VMEM_LIMIT = 56 * 1024 * 1024

SLAB = LANES
N_SLAB = D_MODEL // SLAB
HEADS_PER_SLAB = SLAB // HEAD_B
WKV_CHUNK = 64

DN_NT = (((1,), (1,)), ((), ()))
DN_TN = (((0,), (0,)), ((), ()))


def _const_spec(shape):
    nd = len(shape)
    return pl.BlockSpec(shape, lambda *_: (0,) * nd, pipeline_mode=pl.Buffered(1))


def _params(sem):
    return pltpu.CompilerParams(dimension_semantics=sem, vmem_limit_bytes=VMEM_LIMIT)


def _gelu(x):
    return 0.5 * x * (1.0 + jnp.tanh(0.7978845608028654 * (x + 0.044715 * (x * x * x))))


def _rmsnorm(x, g):
    return x * lax.rsqrt(jnp.mean(x * x, axis=-1, keepdims=True) + RMS_EPS) * g


def _split_bf16(x, n):
    parts, r = [], x
    for i in range(n):
        p = r.astype(BF16)
        parts.append(p)
        if i + 1 < n:
            r = r - p.astype(F32)
    return parts


def _mm(a, b, dn=(((1,), (0,)), ((), ())), pa=1, pb=1):
    acc = None
    bs = _split_bf16(b, pb)
    for i, ai in enumerate(_split_bf16(a, pa)):
        for j, bj in enumerate(bs):
            if i + j >= max(pa, pb):
                continue
            t = lax.dot_general(ai, bj, dn, preferred_element_type=F32)
            acc = t if acc is None else acc + t
    return acc


def _iota(shape, dim):
    return lax.broadcasted_iota(jnp.int32, shape, dim)


def _in_proj_kernel(x_ref, g1_ref, w_ref, lng_ref, lnb_ref, mix_ref, mixb_ref,
                    ya_ref, vn_ref, sz_ref, q_ref, gate_ref):
    tm = x_ref.shape[0]
    hb = _rmsnorm(x_ref[...], g1_ref[...]).astype(BF16)

    def seg(lo, width):
        return jnp.dot(hb, w_ref[:, lo:lo + width], preferred_element_type=F32)

    u = _gelu(seg(0, D_MODEL))
    gv = _gelu(seg(D_MODEL, D_MODEL))
    c = gv - jnp.mean(gv, axis=-1, keepdims=True)
    var = jnp.mean(c * c, axis=-1, keepdims=True)
    vn = c * lax.rsqrt(var + LN_EPS) * lng_ref[...] + lnb_ref[...]
    vn_ref[...] = vn

    causal = _iota((CHUNK_A, CHUNK_A), 1) <= _iota((CHUNK_A, CHUNK_A), 0)
    vnb = vn.astype(BF16)
    gw = D_MODEL // GROUPS_A
    for g in range(GROUPS_A):
        wm = jnp.where(causal, mix_ref[g], 0.0).astype(BF16)
        cols = slice(g * gw, (g + 1) * gw)
        for rb in range(tm // CHUNK_A):
            rows = slice(rb * CHUNK_A, (rb + 1) * CHUNK_A)
            s = jnp.dot(wm, vnb[rows, cols], preferred_element_type=F32) + mixb_ref[:, cols]
            ya_ref[rows, cols] = u[rows, cols] * s

    off = 2 * D_MODEL
    sz_ref[...] = seg(off, P_SHIFT)
    off += P_SHIFT
    q_ref[...] = seg(off, D_MODEL)
    off += D_MODEL
    gate_ref[...] = jax.nn.sigmoid(seg(off, N_BRANCH * D_MODEL))


def _in_proj(x, g1, w_in_bf, lng, lnb, mix, mixb, tm):
    n = x.shape[0]
    row = lambda w: pl.BlockSpec((tm, w), lambda i: (i, 0))
    widths = (D_MODEL, D_MODEL, P_SHIFT, D_MODEL, N_BRANCH * D_MODEL)
    return pl.pallas_call(
        _in_proj_kernel,
        grid=(n // tm,),
        in_specs=[row(D_MODEL), _const_spec((1, D_MODEL)), _const_spec((D_MODEL, P_IN)),
                  _const_spec((1, D_MODEL)), _const_spec((1, D_MODEL)),
                  _const_spec((GROUPS_A, CHUNK_A, CHUNK_A)), _const_spec((CHUNK_A, D_MODEL))],
        out_specs=[row(w) for w in widths],
        out_shape=[jax.ShapeDtypeStruct((n, w), F32) for w in widths],
        compiler_params=_params(("parallel",)),
        name="in_proj",
    )(x, g1, w_in_bf, lng, lnb, mix, mixb)


def _head_sum(x, bd_ones):
    outs = [_mm(x[:, s * SLAB:(s + 1) * SLAB], bd_ones, pa=2, pb=1) for s in range(N_SLAB)]
    return jnp.concatenate(outs, axis=1)


def _head_block_ones():
    same = (_iota((SLAB, SLAB), 0) // HEAD_B) == (_iota((SLAB, SLAB), 1) // HEAD_B)
    return jnp.where(same, 1.0, 0.0).astype(F32)


def _rwkv_prep(ss, w0, ww2, a0, wa2, wg2, k_k, k_a, bd_ones):
    rz = ss[:, 0:D_MODEL]
    kz = ss[:, D_MODEL:2 * D_MODEL]
    vz = ss[:, 2 * D_MODEL:3 * D_MODEL]
    lor = ss[:, 3 * D_MODEL:3 * D_MODEL + LORA_WA]
    gl = ss[:, 3 * D_MODEL + LORA_WA:P_SHIFT]
    z = -(w0 + _mm(jnp.tanh(lor), ww2, pa=2, pb=2))
    softplus = jnp.maximum(z, 0.0) + jnp.log(1.0 + jnp.exp(-jnp.abs(z)))
    lw = -jnp.exp(-softplus - 0.5)
    a = jax.nn.sigmoid(a0 + _mm(lor, wa2, pa=2, pb=2))
    g = _mm(jax.nn.sigmoid(gl), wg2, pa=2, pb=2)
    kk = kz * k_k
    kk = kk * lax.rsqrt(_head_sum(kk * kk, bd_ones) + 1e-12)
    k = kz * (1.0 + (a - 1.0) * k_a)
    return rz, lw, k, vz, -kk, kk * a, g


def _rwkv_prompt_kernel(sz_ref, mu_ref, w0_ref, ww2_ref, a0_ref, wa2_ref, wg2_ref, kk_ref, ka_ref,
                        rk_ref, lng_ref, lnb_ref,
                        yb_ref, st_out_ref,
                        prev_s, st_s, rt_s, at_s, kt_s, bt_s, kh_s, bh_s, v_s, pc_s, y_s):
    tt = sz_ref.shape[0]
    n_chunk = tt // WKV_CHUNK
    i = pl.program_id(1)

    @pl.when(i == 0)
    def _():
        prev_s[...] = jnp.zeros_like(prev_s)
        st_s[...] = jnp.zeros_like(st_s)

    sz = sz_ref[...]
    first = _iota((tt, 1), 0) == 0
    prev = jnp.where(first, prev_s[...], pltpu.roll(sz, 1, 0))
    prev_s[...] = sz[tt - 1:tt, :]
    ss = sz + (prev - sz) * mu_ref[...]

    bd_ones = _head_block_ones()
    r, lw, k, v, av, bv, g = _rwkv_prep(ss, w0_ref[...], ww2_ref[...], a0_ref[...], wa2_ref[...],
                                         wg2_ref[...], kk_ref[...], ka_ref[...], bd_ones)

    c_ = WKV_CHUNK
    lower = jnp.where(_iota((c_, c_), 1) <= _iota((c_, c_), 0), 1.0, 0.0).astype(F32)
    for c in range(n_chunk):
        rows = slice(c * c_, (c + 1) * c_)
        lwc = lw[rows]
        cum = _mm(lower, lwc, pa=1, pb=3)
        last = cum[c_ - 1:c_, :]
        e_in = jnp.exp(cum)
        e_ex = jnp.exp(cum - lwc)
        e_neg = jnp.exp(-cum)
        e_hat = jnp.exp(last - cum)
        pieces = ((rt_s, r[rows] * e_in), (at_s, av[rows] * e_ex), (kt_s, k[rows] * e_neg),
                  (bt_s, bv[rows] * e_neg), (kh_s, k[rows] * e_hat), (bh_s, bv[rows] * e_hat),
                  (v_s, v[rows]))
        e_last = jnp.exp(last)
        for s in range(N_SLAB):
            cols = slice(s * SLAB, (s + 1) * SLAB)
            for ref, val in pieces:
                ref[s, rows, :] = val[:, cols]
            pc_s[c, s] = e_last[:, cols]

    lane_head = _iota((1, SLAB), 1) // HEAD_B
    strict = _iota((c_, c_), 1) < _iota((c_, c_), 0)
    incl = _iota((c_, c_), 1) <= _iota((c_, c_), 0)
    eye_c = jnp.where(_iota((c_, c_), 1) == _iota((c_, c_), 0), 1.0, 0.0).astype(F32)
    eye_s = _iota((SLAB, SLAB), 1) == _iota((SLAB, SLAB), 0)
    same_head = (_iota((SLAB, SLAB), 0) // HEAD_B) == (_iota((SLAB, SLAB), 1) // HEAD_B)
    mm3 = functools.partial(_mm, pa=2, pb=2)

    def chunk_body(c, carry):
        t0 = pl.multiple_of(c * c_, c_)

        def slab_body(s, carry2):
            ld = lambda ref: ref[s, pl.ds(t0, c_), :]
            rt, at, kt, bt, kh, bh, vv = (ld(x) for x in (rt_s, at_s, kt_s, bt_s, kh_s, bh_s, v_s))
            st = st_s[s]
            rhs_u = mm3(at, st)
            y = mm3(rt, st)
            per_head = []
            for p in range(HEADS_PER_SLAB):
                hm = lane_head == p
                atp = jnp.where(hm, at, 0.0)
                rtp = jnp.where(hm, rt, 0.0)
                vp = jnp.where(hm, vv, 0.0)
                a_ab = jnp.where(strict, mm3(atp, bt, DN_NT), 0.0)
                a_ak = jnp.where(strict, mm3(atp, kt, DN_NT), 0.0)
                a_rb = jnp.where(incl, mm3(rtp, bt, DN_NT), 0.0)
                a_rk = jnp.where(incl, mm3(rtp, kt, DN_NT), 0.0)
                inv = eye_c + a_ab
                pw = a_ab
                for _ in range(5):
                    pw = mm3(pw, pw)
                    inv = inv + mm3(inv, pw)
                rhs_u = rhs_u + mm3(a_ak, vp)
                y = y + mm3(a_rk, vp)
                per_head.append((hm, inv, a_rb))
            u = None
            for hm, inv, _ in per_head:
                t = mm3(inv, jnp.where(hm, rhs_u, 0.0))
                u = t if u is None else u + t
            for hm, _, a_rb in per_head:
                y = y + mm3(a_rb, jnp.where(hm, u, 0.0))
            y_s[s, pl.ds(t0, c_), :] = y
            grow = mm3(bh, u, DN_TN) + mm3(kh, vv, DN_TN)
            pc = jnp.broadcast_to(pc_s[c, s], (SLAB, SLAB))
            pc_col = jnp.sum(jnp.where(eye_s, pc, 0.0), axis=1, keepdims=True)
            st_s[s] = pc_col * st + jnp.where(same_head, grow, 0.0)
            return carry2

        lax.fori_loop(0, N_SLAB, slab_body, 0)
        return carry

    lax.fori_loop(0, n_chunk, chunk_body, 0)

    y = jnp.concatenate([y_s[s] for s in range(N_SLAB)], axis=1)
    mu = _head_sum(y, bd_ones) * (1.0 / HEAD_B)
    cen = y - mu
    var = _head_sum(cen * cen, bd_ones) * (1.0 / HEAD_B)
    yn = cen * lax.rsqrt(var + GN_EPS) * lng_ref[...] + lnb_ref[...]
    bonus = _head_sum(r * k * rk_ref[...], bd_ones) * v
    yb_ref[...] = (yn + bonus) * g

    @pl.when(i == pl.num_programs(1) - 1)
    def _():
        st_out_ref[...] = st_s[...]


def _rwkv_prompt(sz, wts, batch, seq, tt):
    row = lambda w: pl.BlockSpec((tt, w), lambda b, i: (b * (seq // tt) + i, 0))
    slab_tile = pltpu.VMEM((N_SLAB, tt, SLAB), F32)
    return pl.pallas_call(
        _rwkv_prompt_kernel,
        grid=(batch, seq // tt),
        in_specs=[row(P_SHIFT)] + [_const_spec(w.shape) for w in wts],
        out_specs=[row(D_MODEL),
                   pl.BlockSpec((None, N_SLAB, SLAB, SLAB), lambda b, i: (b, 0, 0, 0))],
        out_shape=[jax.ShapeDtypeStruct((batch * seq, D_MODEL), F32),
                   jax.ShapeDtypeStruct((batch, N_SLAB, SLAB, SLAB), F32)],
        scratch_shapes=[pltpu.VMEM((1, P_SHIFT), F32), pltpu.VMEM((N_SLAB, SLAB, SLAB), F32)]
        + [slab_tile] * 7
        + [pltpu.VMEM((tt // WKV_CHUNK, N_SLAB, 1, SLAB), F32), slab_tile],
        compiler_params=_params(("arbitrary", "arbitrary")),
        name="rwkv_prompt",
    )(sz, *wts)


def _rwkv_sample_prep_kernel(sz_ref, szp_ref, st_ref, mu_ref, w0_ref, ww2_ref, a0_ref, wa2_ref,
                             wg2_ref, kk_ref, ka_ref,
                             r_ref, w_ref, k_ref, v_ref, a_ref, b_ref, g_ref):
    t = pl.program_id(0)
    sz = sz_ref[...]
    prev = jnp.where(t == 0, st_ref[...], szp_ref[...])
    ss = sz + (prev - sz) * mu_ref[...]
    r, lw, k, v, av, bv, g = _rwkv_prep(ss, w0_ref[...], ww2_ref[...], a0_ref[...], wa2_ref[...],
                                         wg2_ref[...], kk_ref[...], ka_ref[...], _head_block_ones())
    for ref, val in ((r_ref, r), (w_ref, jnp.exp(lw)), (k_ref, k), (v_ref, v), (a_ref, av),
                     (b_ref, bv), (g_ref, g)):
        ref[...] = val.T


def _rwkv_sample_prep(sz_t, state_shift, wts, nb, nt):
    blk = lambda f: pl.BlockSpec((nb, P_SHIFT), f)
    out = pl.BlockSpec((None, D_MODEL, nb), lambda t: (t, 0, 0))
    return pl.pallas_call(
        _rwkv_sample_prep_kernel,
        grid=(nt,),
        in_specs=[blk(lambda t: (t, 0)), blk(lambda t: (jnp.maximum(t - 1, 0), 0)),
                  _const_spec((nb, P_SHIFT))] + [_const_spec(w.shape) for w in wts],
        out_specs=[out] * 7,
        out_shape=[jax.ShapeDtypeStruct((nt, D_MODEL, nb), F32)] * 7,
        compiler_params=_params(("parallel",)),
        name="rwkv_sample_prep",
    )(sz_t, sz_t, state_shift, *wts)


def _rwkv_sample_kernel(s_ref, r_ref, w_ref, k_ref, v_ref, a_ref, b_ref, g_ref, lng_ref, lnb_ref,
                        rk_ref, yo_ref, so_ref, st_s, y_s):
    nt = r_ref.shape[0]
    nb = s_ref.shape[0]
    rows_per = nb // HEAD_B
    n_piece = HEAD_B * HEAD_B // nb
    for q in range(n_piece):
        piece = s_ref[:, q * nb:(q + 1) * nb].T
        st_s[q * rows_per:(q + 1) * rows_per] = piece.reshape(rows_per, HEAD_B, nb)

    def value_row(i, carry):
        st = st_s[i]
        for t in range(nt):
            sa = jnp.sum(st * a_ref[t], axis=0, keepdims=True)
            st = st * w_ref[t] + sa * b_ref[t] + v_ref[t, pl.ds(i, 1), :] * k_ref[t]
            y_s[t, pl.ds(i, 1), :] = jnp.sum(st * r_ref[t], axis=0, keepdims=True)
        st_s[i] = st
        return carry

    lax.fori_loop(0, HEAD_B, value_row, 0)

    for t in range(nt):
        y = y_s[t]
        cen = y - jnp.mean(y, axis=0, keepdims=True)
        var = jnp.mean(cen * cen, axis=0, keepdims=True)
        yn = cen * lax.rsqrt(var + GN_EPS) * lng_ref[...] + lnb_ref[...]
        bonus = jnp.sum(r_ref[t] * k_ref[t] * rk_ref[...], axis=0, keepdims=True) * v_ref[t]
        yo_ref[t] = (yn + bonus) * g_ref[t]

    for q in range(n_piece):
        piece = st_s[q * rows_per:(q + 1) * rows_per].reshape(nb, nb)
        so_ref[:, q * nb:(q + 1) * nb] = piece.T


def _rwkv_sample(state, vecs, lng_col, lnb_col, rk_col, nb, nt):
    hh = HEAD_B * HEAD_B
    vec = pl.BlockSpec((nt, HEAD_B, nb), lambda h: (0, h, 0))
    col = pl.BlockSpec((HEAD_B, 1), lambda h: (h, 0))
    st = pl.BlockSpec((nb, hh), lambda h: (0, h))
    return pl.pallas_call(
        _rwkv_sample_kernel,
        grid=(H_B,),
        in_specs=[st] + [vec] * 7 + [col] * 3,
        out_specs=[vec, st],
        out_shape=[jax.ShapeDtypeStruct((nt, D_MODEL, nb), F32),
                   jax.ShapeDtypeStruct((nb, H_B * hh), F32)],
        scratch_shapes=[pltpu.VMEM((HEAD_B, HEAD_B, nb), F32), pltpu.VMEM((nt, HEAD_B, nb), F32)],
        compiler_params=_params(("parallel",)),
        name="rwkv_sample",
    )(state, *vecs, lng_col, lnb_col, rk_col)


def _mem_kv_kernel(m_ref, g_ref, wk_ref, wv_ref, k_ref, v_ref):
    hm = _rmsnorm(m_ref[...], g_ref[...]).astype(BF16)
    k_ref[...] = jnp.dot(hm, wk_ref[...], preferred_element_type=F32)
    v_ref[...] = jnp.dot(hm, wv_ref[...], preferred_element_type=F32)


def _mem_kv(mem, g, wk_bf, wv_bf, tm):
    n = mem.shape[0]
    row = pl.BlockSpec((tm, D_MODEL), lambda i: (i, 0))
    return pl.pallas_call(
        _mem_kv_kernel,
        grid=(n // tm,),
        in_specs=[row, _const_spec((1, D_MODEL)), _const_spec((D_MODEL, D_MODEL)),
                  _const_spec((D_MODEL, D_MODEL))],
        out_specs=[row, row],
        out_shape=[jax.ShapeDtypeStruct((n, D_MODEL), F32)] * 2,
        compiler_params=_params(("parallel",)),
        name="mem_kv",
    )(mem, g, wk_bf, wv_bf)


def _xattn_kernel(q_ref, k_ref, v_ref, o_ref):
    for h in range(H_C):
        cols = slice(h * DH_C, (h + 1) * DH_C)
        q = q_ref[:, cols].astype(BF16)
        s = lax.dot_general(q, k_ref[:, cols].astype(BF16), DN_NT,
                            preferred_element_type=F32) * (DH_C ** -0.5)
        e = jnp.exp(s - jnp.max(s, axis=-1, keepdims=True))
        p = e / jnp.sum(e, axis=-1, keepdims=True)
        o_ref[:, cols] = jnp.dot(p.astype(BF16), v_ref[:, cols].astype(BF16),
                                 preferred_element_type=F32)


def _xattn(q, mk, mv, tq):
    b, t, _ = q.shape
    n_mem = mk.shape[1]
    qs = pl.BlockSpec((None, tq, D_MODEL), lambda bi, i: (bi, i, 0))
    kv = pl.BlockSpec((None, n_mem, D_MODEL), lambda bi, i: (bi, 0, 0))
    return pl.pallas_call(
        _xattn_kernel,
        grid=(b, t // tq),
        in_specs=[qs, kv, kv],
        out_specs=qs,
        out_shape=jax.ShapeDtypeStruct((b, t, D_MODEL), F32),
        compiler_params=_params(("parallel", "parallel")),
        name="mem_xattn",
    )(q, mk, mv)


def _merge_mlp_kernel(x_ref, ya_ref, yb_ref, yc_ref, gate_ref, wbr_ref, wout_ref, n2_ref, wup_ref,
                      wdn_ref, nf_ref, y_ref):
    merged = None
    for n, br in enumerate((ya_ref, yb_ref, yc_ref)):
        proj = jnp.dot(br[...].astype(BF16), wbr_ref[n], preferred_element_type=F32)
        t = proj * gate_ref[:, n * D_MODEL:(n + 1) * D_MODEL]
        merged = t if merged is None else merged + t
    x1 = x_ref[...] + jnp.dot(merged.astype(BF16), wout_ref[...], preferred_element_type=F32)
    h2 = _rmsnorm(x1, n2_ref[...]).astype(BF16)
    up = jnp.dot(h2, wup_ref[...], preferred_element_type=F32)
    act = jnp.square(jnp.maximum(up, 0.0)).astype(BF16)
    x2 = x1 + jnp.dot(act, wdn_ref[...], preferred_element_type=F32)
    y_ref[...] = _rmsnorm(x2, nf_ref[...])


def _merge_mlp(x, ya, yb, yc, gates, wbr_bf, wout_bf, n2, wup_bf, wdn_bf, nf, tm):
    n = x.shape[0]
    row = lambda w: pl.BlockSpec((tm, w), lambda i: (i, 0))
    return pl.pallas_call(
        _merge_mlp_kernel,
        grid=(n // tm,),
        in_specs=[row(D_MODEL)] * 4 + [row(N_BRANCH * D_MODEL)]
        + [_const_spec(w.shape) for w in (wbr_bf, wout_bf, n2, wup_bf, wdn_bf, nf)],
        out_specs=row(D_MODEL),
        out_shape=jax.ShapeDtypeStruct((n, D_MODEL), F32),
        compiler_params=_params(("parallel",)),
        name="merge_mlp",
    )(x, ya, yb, yc, gates, wbr_bf, wout_bf, n2, wup_bf, wdn_bf, nf)


def kernel(x_prompt, x_sample, mem_prompt, cache_mem_k, cache_mem_v, state_shift, state_wkv,
           norm1_g, w_in, shift_mu, ln_v_g, ln_v_b, sg_w, sg_b, w0, w_w2, a0, w_a2, w_g2,
           k_k, k_a, r_k, lnx_g, lnx_b, mem_norm_g, w_mem_k, w_mem_v, w_branch, w_out,
           norm2_g, w_up, w_down, norm_f_g):
    assert w_in.shape[0] == 1, "one layer"
    bp, seq, _ = x_prompt.shape
    bs, nt, _ = x_sample.shape
    n_mem = mem_prompt.shape[1]
    l = 0
    row = lambda a: a.reshape(1, -1)

    w_in_bf = w_in[l].astype(BF16)
    wbr_bf, wout_bf = w_branch[l].astype(BF16), w_out[l].astype(BF16)
    wup_bf, wdn_bf = w_up[l].astype(BF16), w_down[l].astype(BF16)
    wk_bf, wv_bf = w_mem_k[l].astype(BF16), w_mem_v[l].astype(BF16)
    half = LORA_WA // 2
    ww2 = jnp.concatenate([w_w2[l], jnp.zeros((half, D_MODEL), F32)], axis=0)
    wa2 = jnp.concatenate([jnp.zeros((half, D_MODEL), F32), w_a2[l]], axis=0)
    rwkv_w = (row(w0[l]), ww2, row(a0[l]), wa2, w_g2[l], row(k_k[l]), row(k_a[l]))
    gw = D_MODEL // GROUPS_A
    mix_p = sg_w[l]
    mixb_p = jnp.repeat(sg_b[l].T, gw, axis=1)
    per_blk = CHUNK_A // nt
    eye = jnp.eye(per_blk, dtype=F32)
    mix_s = jax.vmap(lambda m: jnp.kron(eye, m))(sg_w[l][:, :nt, :nt])
    mixb_s = jnp.tile(jnp.repeat(sg_b[l].T[:nt], gw, axis=1), (per_blk, 1))

    xp = x_prompt.reshape(bp * seq, D_MODEL)
    ya_p, _, sz_p, q_p, gate_p = _in_proj(xp, row(norm1_g[l]), w_in_bf, row(ln_v_g[l]),
                                          row(ln_v_b[l]), mix_p, mixb_p, tm=256)
    yb_p, st_p = _rwkv_prompt(sz_p, (row(shift_mu[l]),) + rwkv_w
                              + (row(r_k[l]), row(lnx_g[l]), row(lnx_b[l])), bp, seq, tt=256)
    mk_p, mv_p = _mem_kv(mem_prompt.reshape(bp * n_mem, D_MODEL), row(mem_norm_g[l]), wk_bf, wv_bf,
                         tm=256)
    yc_p = _xattn(q_p.reshape(bp, seq, D_MODEL), mk_p.reshape(bp, n_mem, D_MODEL),
                  mv_p.reshape(bp, n_mem, D_MODEL), tq=256)
    y_p = _merge_mlp(xp, ya_p, yb_p, yc_p.reshape(bp * seq, D_MODEL), gate_p, wbr_bf, wout_bf,
                     row(norm2_g[l]), wup_bf, wdn_bf, row(norm_f_g), tm=256)

    xs = x_sample.reshape(bs * nt, D_MODEL)
    ya_s, vn_s, sz_s, q_s, gate_s = _in_proj(xs, row(norm1_g[l]), w_in_bf, row(ln_v_g[l]),
                                             row(ln_v_b[l]), mix_s, mixb_s, tm=256)
    sz_t = sz_s.reshape(bs, nt, P_SHIFT).transpose(1, 0, 2).reshape(nt * bs, P_SHIFT)
    vecs = _rwkv_sample_prep(sz_t, state_shift[l], (row(shift_mu[l]),) + rwkv_w, bs, nt)
    col = lambda a: a.reshape(-1, 1)
    yb_t, wkv_s = _rwkv_sample(state_wkv[l].reshape(bs, H_B * HEAD_B * HEAD_B), vecs,
                               col(lnx_g[l]), col(lnx_b[l]), col(r_k[l]), bs, nt)
    yb_s = yb_t.transpose(2, 0, 1).reshape(bs * nt, D_MODEL)
    yc_s = _xattn(q_s.reshape(bs, nt, D_MODEL), cache_mem_k[l].reshape(bs, n_mem, D_MODEL),
                  cache_mem_v[l].reshape(bs, n_mem, D_MODEL), tq=nt)
    y_s = _merge_mlp(xs, ya_s, yb_s, yc_s.reshape(bs * nt, D_MODEL), gate_s, wbr_bf, wout_bf,
                     row(norm2_g[l]), wup_bf, wdn_bf, row(norm_f_g), tm=256)

    st6 = st_p.reshape(bp, N_SLAB, HEADS_PER_SLAB, HEAD_B, HEADS_PER_SLAB, HEAD_B)
    diag = jnp.stack([st6[:, :, p, :, p, :] for p in range(HEADS_PER_SLAB)], axis=2)
    wkv_p = diag.reshape(bp, H_B, HEAD_B, HEAD_B).swapaxes(-1, -2)

    return (y_p.reshape(bp, seq, D_MODEL),
            y_s.reshape(bs, nt, D_MODEL),
            sz_p.reshape(bp, seq, P_SHIFT)[:, -1][None],
            wkv_p[None],
            mk_p.reshape(1, bp, n_mem, H_C, DH_C),
            mv_p.reshape(1, bp, n_mem, H_C, DH_C),
            sz_s.reshape(bs, nt, P_SHIFT)[:, -1][None],
            wkv_s.reshape(1, bs, H_B, HEAD_B, HEAD_B),
            vn_s.reshape(1, bs, nt, D_MODEL))
```
